```python
import jax
import jax.numpy as jnp
from jax import lax
import numpy as np

D_MODEL = 1024
BATCH = 8
SEQ = 4096
DEPTH = 2

GRID_W = 64
CTX_LEN = 256
EPS = 1e-6

FOURIER_GROUPS = 4
FOURIER_DIM = D_MODEL // 2
FOURIER_GROUP_DIM = FOURIER_DIM // FOURIER_GROUPS

SSD_D_INNER = D_MODEL
SSD_HEAD_DIM = 64
SSD_HEADS = SSD_D_INNER // SSD_HEAD_DIM
SSD_GROUPS = 2
SSD_STATE = 128
SSD_GN = SSD_GROUPS * SSD_STATE
SSD_CONV = 5
SSD_CONV_DIM = SSD_D_INNER + 2 * SSD_GN
SSD_CHUNK = 128

EVEN_IN = FOURIER_DIM + SSD_D_INNER + SSD_CONV_DIM + 2 * SSD_HEADS
EVEN_MIX = FOURIER_DIM + SSD_D_INNER

SGU_DIM = D_MODEL
SGU_GROUPS = 4
SGU_GROUP_DIM = SGU_DIM // SGU_GROUPS
CHUNK_ROWS = 2
SGU_CHUNK = CHUNK_ROWS * GRID_W

POOL_WINDOWS = (2, 4, 8, 16)
POOL_GROUPS = len(POOL_WINDOWS)
POOL_DIM = D_MODEL // 2
POOL_GROUP_DIM = POOL_DIM // POOL_GROUPS

ODD_IN = 2 * SGU_DIM + POOL_DIM
ODD_MIX = SGU_DIM + POOL_DIM

D_FF = 4 * D_MODEL

N_EVEN = (DEPTH + 1) // 2
N_ODD = DEPTH // 2

kernel_name = 'hybrid_fourier_ssd_sgu_pool_dit'


def rms_norm(x, g):
    xf = x.astype(jnp.float32)
    y = xf * lax.rsqrt(jnp.mean(xf * xf, axis=-1, keepdims=True) + EPS)
    return (y * g.astype(jnp.float32)).astype(x.dtype)


def layer_norm(x, g, b):
    xf = x.astype(jnp.float32)
    xc = xf - jnp.mean(xf, axis=-1, keepdims=True)
    y = xc * lax.rsqrt(jnp.mean(xc * xc, axis=-1, keepdims=True) + EPS)
    return (y * g.astype(jnp.float32) + b.astype(jnp.float32)).astype(x.dtype)


def adaln(cond, w, b, n):
    m = (jax.nn.silu(cond) @ w + b)[..., None, :]
    return jnp.split(m, n, axis=-1)


def modulate(x, g, shift, scale):
    return rms_norm(x, g) * (1 + scale) + shift


def squared_relu_mlp(h, w1, w2):
    return jnp.square(jax.nn.relu(h @ w1)) @ w2


def context_needed_after(layer):
    return any(j % 2 == 0 for j in range(layer + 1, DEPTH))


def fourier_mix(u):
    b_, l, _ = u.shape
    ug = u.astype(jnp.float32).reshape(b_, l, FOURIER_GROUPS, FOURIER_GROUP_DIM)
    y = jnp.fft.fft2(ug, axes=(1, 3), norm='ortho').real
    return y.reshape(b_, l, FOURIER_DIM).astype(u.dtype)


def depthwise_conv_centred(x, w, b):
    pad = w.shape[0] // 2
    y = lax.conv_general_dilated(x, w[:, None, :], window_strides=(1,), padding=[(pad, pad)],
                                 dimension_numbers=('NWC', 'WIO', 'NWC'),
                                 feature_group_count=x.shape[-1])
    return y + b


def zero_states(b_):
    z = jnp.zeros((b_, SSD_HEADS, SSD_HEAD_DIM, SSD_STATE), jnp.float32)
    return (z, z)


def ssd_inputs(xbc_dt, conv_w, conv_b, dt_bias):
    xbc = jax.nn.silu(depthwise_conv_centred(xbc_dt[..., :SSD_CONV_DIM], conv_w, conv_b))
    b_, l, _ = xbc.shape
    xs = xbc[..., :SSD_D_INNER].reshape(b_, l, SSD_HEADS, SSD_HEAD_DIM)
    bm = xbc[..., SSD_D_INNER:SSD_D_INNER + SSD_GN].reshape(b_, l, SSD_GROUPS, SSD_STATE)
    cm = xbc[..., SSD_D_INNER + SSD_GN:].reshape(b_, l, SSD_GROUPS, SSD_STATE)
    dt_raw = xbc_dt[..., SSD_CONV_DIM:].astype(jnp.float32).reshape(b_, l, 2, SSD_HEADS)
    dt = jax.nn.softplus(dt_raw + dt_bias.astype(jnp.float32))
    return xs, bm, cm, dt


def ssd_scan(xs, dt, a, bm, cm, h0, return_y):
    b_, l, nh, hp = xs.shape
    ng, ns = bm.shape[2], bm.shape[3]
    hg = nh // ng
    q = SSD_CHUNK
    nc = l // q
    f32 = jnp.float32
    x = xs.astype(f32).reshape(b_, nc, q, ng, hg, hp)
    dtc = dt.reshape(b_, nc, q, ng, hg)
    bc = bm.astype(f32).reshape(b_, nc, q, ng, ns)
    cum = jnp.cumsum(dtc * a.astype(f32).reshape(ng, hg), axis=2)
    xdt = x * dtc[..., None]
    to_end = jnp.exp(cum[:, :, -1:] - cum)
    states = jnp.einsum('bcjgn,bcjgh,bcjghp->bcghpn', bc, to_end, xdt)
    chunk_decay = jnp.exp(cum[:, :, -1])

    def step(h, inp):
        st, dec = inp
        return h * dec[..., None, None] + st, (h if return_y else None)

    h_last, h_prev = lax.scan(step, h0.reshape(b_, ng, hg, hp, ns),
                              (jnp.moveaxis(states, 1, 0), jnp.moveaxis(chunk_decay, 1, 0)))
    h_last = h_last.reshape(b_, nh, hp, ns)
    if not return_y:
        return None, h_last
    cc = cm.astype(f32).reshape(b_, nc, q, ng, ns)
    seg = cum[:, :, :, None] - cum[:, :, None, :]
    lower = jnp.tril(jnp.ones((q, q), dtype=bool))[None, None, :, :, None, None]
    decay = jnp.exp(jnp.where(lower, seg, -jnp.inf))
    cb = jnp.einsum('bcign,bcjgn->bcijg', cc, bc)
    y_diag = jnp.einsum('bcijgh,bcjghp->bcighp', cb[..., None] * decay, xdt)
    h_prev = jnp.moveaxis(h_prev, 0, 1)
    y_off = jnp.einsum('bcign,bcghpn->bcighp', cc, h_prev) * jnp.exp(cum)[..., None]
    return (y_diag + y_off).reshape(b_, l, nh, hp), h_last


def ssd_bidirectional(xs, bm, cm, dt, a_log, h0, return_y):
    a = -jnp.exp(a_log.astype(jnp.float32))
    rev = lambda t: jnp.flip(t, axis=1)
    y_f, h_f = ssd_scan(xs, dt[:, :, 0], a[0], bm, cm, h0[0], return_y)
    y_b, h_b = ssd_scan(rev(xs), rev(dt[:, :, 1]), a[1], rev(bm), rev(cm), h0[1], return_y)
    y = (y_f + rev(y_b)) if return_y else None
    return y, (h_f, h_b)


def group_rms_norm(y, g):
    b_, l, d = y.shape
    yg = y.reshape(b_, l, SSD_GROUPS, d // SSD_GROUPS)
    yg = yg * lax.rsqrt(jnp.mean(yg * yg, axis=-1, keepdims=True) + EPS)
    return yg.reshape(b_, l, d) * g.astype(jnp.float32)


def even_mixer(h, w_in, conv_w, conv_b, dt_bias, a_log, d_skip, ssd_norm_g, w_out, h0):
    proj = h @ w_in
    f_in = proj[..., :FOURIER_DIM]
    z = proj[..., FOURIER_DIM:FOURIER_DIM + SSD_D_INNER]
    xs, bm, cm, dt = ssd_inputs(proj[..., FOURIER_DIM + SSD_D_INNER:], conv_w, conv_b, dt_bias)
    y, states = ssd_bidirectional(xs, bm, cm, dt, a_log, h0, True)
    y = y + d_skip.astype(jnp.float32)[:, None] * xs.astype(jnp.float32)
    b_, l = y.shape[0], y.shape[1]
    y = y.reshape(b_, l, SSD_D_INNER) * jax.nn.silu(z.astype(jnp.float32))
    y = group_rms_norm(y, ssd_norm_g).astype(h.dtype)
    out = jnp.concatenate([fourier_mix(f_in), y], axis=-1) @ w_out
    return out, states


def context_scan_states(hc, w_in, conv_w, conv_b, dt_bias, a_log):
    proj = hc @ w_in[:, FOURIER_DIM + SSD_D_INNER:]
    xs, bm, cm, dt = ssd_inputs(proj, conv_w, conv_b, dt_bias)
    _, states = ssd_bidirectional(xs, bm, cm, dt, a_log, zero_states(hc.shape[0]), False)
    return states


def multiscale_pool(p):
    b_, l, _ = p.shape
    pf = p.astype(jnp.float32).reshape(b_, l, POOL_GROUPS, POOL_GROUP_DIM)
    cs = jnp.concatenate([jnp.zeros_like(pf[:, :1]), jnp.cumsum(pf, axis=1)], axis=1)
    pos = jnp.arange(l)
    outs = []
    for gi, w in enumerate(POOL_WINDOWS):
        r = w // 2
        lo = jnp.maximum(pos - r, 0)
        hi = jnp.minimum(pos + r + 1, l)
        csg = cs[:, :, gi]
        mean = (csg[:, hi] - csg[:, lo]) / (hi - lo).astype(jnp.float32)[None, :, None]
        outs.append(mean - pf[:, :, gi])
    return jnp.stack(outs, axis=2).astype(p.dtype)


def odd_mixer(h, w_in, sgu_ln_g, sgu_ln_b, w_s, b_s, pool_w, pool_scale, w_out, n_chunks):
    proj = h @ w_in
    uv = jax.nn.gelu(proj[..., :2 * SGU_DIM])
    u, v = uv[..., :SGU_DIM], uv[..., SGU_DIM:]
    v = layer_norm(v, sgu_ln_g, sgu_ln_b)
    b_, l, _ = v.shape
    vc = v.reshape(b_, n_chunks, SGU_CHUNK, SGU_GROUPS, SGU_GROUP_DIM)
    v_mix = jnp.einsum('gij,bcjgd->bcigd', w_s, vc) + jnp.swapaxes(b_s, 0, 1)[:, :, None]
    sgu = u * v_mix.reshape(b_, l, SGU_DIM)
    pooled = multiscale_pool(proj[..., 2 * SGU_DIM:])
    pool_out = jnp.einsum('blgd,gde->blge', pooled, pool_w).reshape(b_, l, POOL_DIM) * pool_scale
    return jnp.concatenate([sgu, pool_out], axis=-1) @ w_out


def setup_inputs(seed: int = 0) -> dict:
    key = jax.random.key(seed)
    keys = iter(jax.random.split(key, 32))
    f32 = jnp.float32

    def nrm(shape, scale):
        return jax.random.normal(next(keys), shape, f32) * scale

    def gain(shape):
        return 1.0 + nrm(shape, 0.02)

    dt0 = jnp.exp(jax.random.uniform(next(keys), (N_EVEN, 2, SSD_HEADS), f32,
                                     float(np.log(1e-3)), float(np.log(1e-1))))
    return {
        'x': nrm((BATCH, SEQ, D_MODEL), 1.0),
        'c': nrm((BATCH, D_MODEL), 1.0),
        'ctx': nrm((BATCH, CTX_LEN, D_MODEL), 1.0),
        'c_ctx': nrm((D_MODEL,), 1.0),
        'ada_w': nrm((DEPTH, D_MODEL, 6 * D_MODEL), 0.5 * D_MODEL ** -0.5),
        'ada_b': nrm((DEPTH, 6 * D_MODEL), 0.02),
        'mix_norm_g': gain((DEPTH, D_MODEL)),
        'ffn_norm_g': gain((DEPTH, D_MODEL)),
        'ffn_w1': nrm((DEPTH, D_MODEL, D_FF), D_MODEL ** -0.5),
        'ffn_w2': nrm((DEPTH, D_FF, D_MODEL), D_FF ** -0.5),
        'ev_w_in': nrm((N_EVEN, D_MODEL, EVEN_IN), D_MODEL ** -0.5),
        'ev_conv_w': nrm((N_EVEN, SSD_CONV, SSD_CONV_DIM), SSD_CONV ** -0.5),
        'ev_conv_b': nrm((N_EVEN, SSD_CONV_DIM), 0.02),
        'ev_dt_bias': dt0 + jnp.log(-jnp.expm1(-dt0)),
        'ev_a_log': jnp.log(jax.random.uniform(next(keys), (N_EVEN, 2, SSD_HEADS), f32, 1.0, 16.0)),
        'ev_d_skip': gain((N_EVEN, SSD_HEADS)),
        'ev_ssd_norm_g': gain((N_EVEN, SSD_D_INNER)),
        'ev_w_out': nrm((N_EVEN, EVEN_MIX, D_MODEL), EVEN_MIX ** -0.5),
        'od_w_in': nrm((N_ODD, D_MODEL, ODD_IN), D_MODEL ** -0.5),
        'od_sgu_ln_g': gain((N_ODD, SGU_DIM)),
        'od_sgu_ln_b': nrm((N_ODD, SGU_DIM), 0.02),
        'od_w_s': nrm((N_ODD, SGU_GROUPS, SGU_CHUNK, SGU_CHUNK), SGU_CHUNK ** -0.5),
        'od_b_s': gain((N_ODD, SGU_GROUPS, SGU_CHUNK)),
        'od_pool_w': nrm((N_ODD, POOL_GROUPS, POOL_GROUP_DIM, POOL_GROUP_DIM), POOL_GROUP_DIM ** -0.5),
        'od_pool_scale': gain((N_ODD, POOL_DIM)),
        'od_w_out': nrm((N_ODD, ODD_MIX, D_MODEL), ODD_MIX ** -0.5),
        'final_g': gain((D_MODEL,)),
    }


def reference(x, c, ctx, c_ctx, ada_w, ada_b, mix_norm_g, ffn_norm_g, ffn_w1, ffn_w2,
              ev_w_in, ev_conv_w, ev_conv_b, ev_dt_bias, ev_a_log, ev_d_skip, ev_ssd_norm_g, ev_w_out,
              od_w_in, od_sgu_ln_g, od_sgu_ln_b, od_w_s, od_b_s, od_pool_w, od_pool_scale, od_w_out,
              final_g):
    rows = x.shape[1] // GRID_W
    n_lat_chunks = rows // CHUNK_ROWS
    n_ctx_chunks = ctx.shape[1] // SGU_CHUNK
    lat, cur = x, ctx
    for layer in range(DEPTH):
        ctx_out = context_needed_after(layer)
        s_m, sc_m, g_m, s_f, sc_f, g_f = adaln(c, ada_w[layer], ada_b[layer], 6)
        h = modulate(lat, mix_norm_g[layer], s_m, sc_m)
        if ctx_out:
            cmod = adaln(c_ctx, ada_w[layer], ada_b[layer], 6)
            hc = modulate(cur, mix_norm_g[layer], cmod[0], cmod[1])
        elif layer % 2 == 0:
            cmod = adaln(c_ctx, ada_w[layer][:, :2 * D_MODEL], ada_b[layer][:2 * D_MODEL], 2)
            hc = modulate(cur, mix_norm_g[layer], cmod[0], cmod[1])
        if layer % 2 == 0:
            e = layer // 2
            even_p = (ev_w_in[e], ev_conv_w[e], ev_conv_b[e], ev_dt_bias[e], ev_a_log[e],
                      ev_d_skip[e], ev_ssd_norm_g[e], ev_w_out[e])
            if ctx_out:
                ctx_mix, states = even_mixer(hc, *even_p, zero_states(cur.shape[0]))
            else:
                states = context_scan_states(hc, *even_p[:5])
            lat_mix, _ = even_mixer(h, *even_p, states)
        else:
            o = layer // 2
            odd_p = (od_w_in[o], od_sgu_ln_g[o], od_sgu_ln_b[o], od_w_s[o], od_b_s[o],
                     od_pool_w[o], od_pool_scale[o], od_w_out[o])
            lat_mix = odd_mixer(h, *odd_p, n_lat_chunks)
            if ctx_out:
                ctx_mix = odd_mixer(hc, *odd_p, n_ctx_chunks)
        lat = lat + g_m * lat_mix
        lat = lat + g_f * squared_relu_mlp(modulate(lat, ffn_norm_g[layer], s_f, sc_f),
                                           ffn_w1[layer], ffn_w2[layer])
        if ctx_out:
            cur = cur + cmod[2] * ctx_mix
            cur = cur + cmod[5] * squared_relu_mlp(modulate(cur, ffn_norm_g[layer], cmod[3], cmod[4]),
                                                   ffn_w1[layer], ffn_w2[layer])
    return rms_norm(lat, final_g)
```

```python
import functools
import math

import numpy as np
import jax
import jax.numpy as jnp
from jax import lax
from jax.experimental import pallas as pl
from jax.experimental.pallas import tpu as pltpu

F32 = jnp.float32
BF16 = jnp.bfloat16
EPS = 1e-6

FOURIER_GROUPS = 4
SSD_HEAD_DIM = 64
SSD_GROUPS = 2
SSD_STATE = 128
SSD_CONV = 5
SGU_GROUPS = 4
SGU_CHUNK = 128
POOL_WINDOWS = (2, 4, 8, 16)

LANES = 128
CHUNK = 128
HALO = 16
VMEM_LIMIT = 56 * 1024 * 1024


def _cparams(sem):
    return pltpu.CompilerParams(dimension_semantics=sem, vmem_limit_bytes=VMEM_LIMIT)


def _const_spec(shape):
    nd = len(shape)
    return pl.BlockSpec(shape, lambda *_: (0,) * nd, pipeline_mode=pl.Buffered(1))


def _dot(a, b):
    return jnp.dot(a, b, preferred_element_type=F32)


def _sigmoid(x):
    return 1.0 / (1.0 + jnp.exp(-x))


def _modulate(x, g, shift, scale):
    y = x * lax.rsqrt(jnp.mean(x * x, axis=-1, keepdims=True) + EPS)
    return (y * g) * (1.0 + scale) + shift


def _ada_kernel(cond_ref, w_ref, b_ref, o_ref):
    c = cond_ref[...]
    s = c * _sigmoid(c)
    o_ref[0] = jnp.dot(s, w_ref[0], preferred_element_type=F32,
                       precision=lax.Precision.HIGHEST) + b_ref[0]


def _ada_call(cond, ada_w, ada_b):
    depth, d, n = ada_w.shape
    rows = cond.shape[0]
    tn = 1536
    return pl.pallas_call(
        _ada_kernel,
        grid=(depth, n // tn),
        in_specs=[pl.BlockSpec((rows, d), lambda l, j: (0, 0)),
                  pl.BlockSpec((1, d, tn), lambda l, j: (l, 0, j)),
                  pl.BlockSpec((1, 1, tn), lambda l, j: (l, 0, j))],
        out_specs=pl.BlockSpec((1, rows, tn), lambda l, j: (l, 0, j)),
        out_shape=jax.ShapeDtypeStruct((depth, rows, n), F32),
        compiler_params=_cparams(("parallel", "parallel")),
        name="ada_mod",
    )(cond, ada_w, ada_b.reshape(depth, 1, n))


def _evin_kernel(x_ref, xp_ref, xn_ref, sh_ref, sc_ref, g_ref, *rest, tm, with_fz):
    if with_fz:
        (wf_ref, wz_ref, wx_ref, wd_ref, cw_ref, cb_ref, dtb_ref,
         f_ref, z_ref, xbc_ref, dt_ref, hext_ref, raw_ref) = rest
    else:
        (wx_ref, wd_ref, cw_ref, cb_ref, dtb_ref,
         xbc_ref, dt_ref, hext_ref, raw_ref) = rest
    i = pl.program_id(1)
    last = pl.num_programs(1) - 1
    g, sh, sc = g_ref[...], sh_ref[0], sc_ref[0]

    h = _modulate(x_ref[0], g, sh, sc).astype(BF16)
    hp = _modulate(xp_ref[0, 0], g, sh, sc)
    hn = _modulate(xn_ref[0, 0], g, sh, sc)
    hext_ref[0:HALO] = jnp.where(i > 0, hp, 0.0).astype(BF16)
    hext_ref[HALO:HALO + tm] = h
    hext_ref[HALO + tm:] = jnp.where(i < last, hn, 0.0).astype(BF16)

    if with_fz:
        f_ref[0] = _dot(h, wf_ref[...]).astype(BF16)
        z_ref[0] = _dot(h, wz_ref[...]).astype(BF16)
    dtr = _dot(h, wd_ref[...]) + dtb_ref[...]
    dt_ref[0] = jnp.maximum(dtr, 0.0) + jnp.log1p(jnp.exp(-jnp.abs(dtr)))

    raw_ref[...] = _dot(hext_ref[...], wx_ref[...])
    ncol = raw_ref.shape[1]
    pad = SSD_CONV // 2
    for c0 in range(0, ncol, 512):
        for r0 in range(0, tm, 128):
            acc = cb_ref[:, c0:c0 + 512]
            for k in range(SSD_CONV):
                acc = acc + cw_ref[k:k + 1, c0:c0 + 512] * raw_ref[pl.ds(HALO - pad + k + r0, 128), c0:c0 + 512]
            xbc_ref[0, r0:r0 + 128, c0:c0 + 512] = (acc * _sigmoid(acc)).astype(BF16)


def _evin_call(x, shift, scale, g, wf, wz, wx, wd, cw, cb, dtb, *, tm, with_fz):
    b, l, d = x.shape
    nt = l // tm
    th = tm // HALO
    nh = l // HALO
    x4 = x.reshape(b, nh, HALO, d)
    per_batch = shift.shape[0] > 1
    mod_idx = (lambda bb, i: (bb, 0, 0)) if per_batch else (lambda bb, i: (0, 0, 0))
    ncol = wx.shape[1]
    in_specs = [
        pl.BlockSpec((1, tm, d), lambda bb, i: (bb, i, 0)),
        pl.BlockSpec((1, 1, HALO, d), lambda bb, i: (bb, jnp.maximum(i * th - 1, 0), 0, 0)),
        pl.BlockSpec((1, 1, HALO, d), lambda bb, i: (bb, jnp.minimum((i + 1) * th, nh - 1), 0, 0)),
        pl.BlockSpec((1, 1, d), mod_idx),
        pl.BlockSpec((1, 1, d), mod_idx),
        _const_spec((1, d)),
    ]
    args = [x, x4, x4, shift, scale, g]
    if with_fz:
        in_specs += [_const_spec(wf.shape), _const_spec(wz.shape)]
        args += [wf, wz]
    in_specs += [_const_spec(wx.shape), _const_spec(wd.shape), _const_spec(cw.shape),
                 _const_spec(cb.shape), _const_spec(dtb.shape)]
    args += [wx, wd, cw, cb, dtb]
    out_specs, out_shape = [], []
    if with_fz:
        out_specs += [pl.BlockSpec((1, tm, wf.shape[1]), lambda bb, i: (bb, i, 0)),
                      pl.BlockSpec((1, tm, wz.shape[1]), lambda bb, i: (bb, i, 0))]
        out_shape += [jax.ShapeDtypeStruct((b, l, wf.shape[1]), BF16),
                      jax.ShapeDtypeStruct((b, l, wz.shape[1]), BF16)]
    out_specs += [pl.BlockSpec((1, tm, ncol), lambda bb, i: (bb, i, 0)),
                  pl.BlockSpec((1, tm, LANES), lambda bb, i: (bb, i, 0))]
    out_shape += [jax.ShapeDtypeStruct((b, l, ncol), BF16),
                  jax.ShapeDtypeStruct((b, l, LANES), F32)]
    return pl.pallas_call(
        functools.partial(_evin_kernel, tm=tm, with_fz=with_fz),
        grid=(b, nt),
        in_specs=in_specs,
        out_specs=out_specs,
        out_shape=out_shape,
        scratch_shapes=[pltpu.VMEM((tm + 2 * HALO, d), BF16),
                        pltpu.VMEM((tm + 2 * HALO, ncol), F32)],
        compiler_params=_cparams(("parallel", "parallel")),
        name="even_in_proj" if with_fz else "ctx_in_proj",
    )(*args)


def _dft_kernel(c_ref, s_ref, u_ref, wc_ref, o_ref):
    u = u_ref[0]
    p = _dot(c_ref[...], u).astype(BF16)
    q = _dot(s_ref[...], u).astype(BF16)
    o_ref[0] = _dot(jnp.concatenate([p, q], axis=1), wc_ref[...]).astype(BF16)


def _dft_call(u, cmat, smat, wc, *, tk):
    b, l, n = u.shape
    return pl.pallas_call(
        _dft_kernel,
        grid=(l // tk, b),
        in_specs=[pl.BlockSpec((tk, l), lambda j, bb: (j, 0)),
                  pl.BlockSpec((tk, l), lambda j, bb: (j, 0)),
                  pl.BlockSpec((1, l, n), lambda j, bb: (bb, 0, 0)),
                  _const_spec(wc.shape)],
        out_specs=pl.BlockSpec((1, tk, n), lambda j, bb: (bb, j, 0)),
        out_shape=jax.ShapeDtypeStruct((b, l, n), BF16),
        compiler_params=_cparams(("parallel", "arbitrary")),
        name="fourier_dft",
    )(cmat, smat, u, wc)


def _dft_tables(l, n, groups):
    r = int(math.isqrt(l))
    assert r * r == l
    t = np.arange(l)
    a = np.arange(r)
    ang_a = 2.0 * np.pi * ((r * a[:, None] * t[None, :]) % l) / l
    ang_b = 2.0 * np.pi * ((a[:, None] * t[None, :]) % l) / l
    ca, sa = jnp.asarray(np.cos(ang_a), F32), jnp.asarray(np.sin(ang_a), F32)
    cb, sb = jnp.asarray(np.cos(ang_b), F32), jnp.asarray(np.sin(ang_b), F32)
    cmat = (ca[:, None, :] * cb[None, :, :] - sa[:, None, :] * sb[None, :, :]).reshape(l, l).astype(BF16)
    smat = (sa[:, None, :] * cb[None, :, :] + ca[:, None, :] * sb[None, :, :]).reshape(l, l).astype(BF16)
    d = n // groups
    j = np.arange(d)
    ang_d = 2.0 * np.pi * ((j[:, None] * j[None, :]) % d) / d
    scale = 1.0 / math.sqrt(l * d)
    eye = np.eye(groups)
    wc = np.concatenate([np.kron(eye, np.cos(ang_d)), -np.kron(eye, np.sin(ang_d))], axis=0) * scale
    return cmat, smat, jnp.asarray(wc, F32).astype(BF16)


def _ssd_kernel(*refs, direction, nchunk, want_y, has_h0, finalize, d_inner):
    refs = list(refs)
    xbc_ref, dt_ref, alog_ref = refs[:3]
    pos = 3
    h0_ref = None
    if has_h0:
        h0_ref = refs[pos]
        pos += 1
    if finalize:
        yf_ref, z_ref, dsk_ref, ng_ref = refs[pos:pos + 4]
        pos += 4
    out_ref = refs[pos]
    s_ref = refs[pos + 1]
    ybuf_ref = refs[pos + 2] if want_y else None

    step = pl.program_id(1)
    nstate = SSD_STATE
    gcols = d_inner // SSD_GROUPS
    npair = gcols // LANES
    nheads = d_inner // SSD_HEAD_DIM

    @pl.when(step == 0)
    def _():
        if has_h0:
            s_ref[...] = h0_ref[0]
        else:
            s_ref[...] = jnp.zeros_like(s_ref)

    lane = lax.broadcasted_iota(jnp.int32, (1, LANES), 1)
    a_row = jnp.where(lane < 2 * nheads, -jnp.exp(alog_ref[...]), 0.0)
    ri = lax.broadcasted_iota(jnp.int32, (CHUNK, CHUNK), 0)
    ci = lax.broadcasted_iota(jnp.int32, (CHUNK, CHUNK), 1)
    mask = (ci <= ri) if direction == 0 else (ci >= ri)
    tri = jnp.where(mask, 1.0, 0.0).astype(BF16)
    lo_half = ci < SSD_HEAD_DIM
    lo_row = lane < SSD_HEAD_DIM

    def bcast(col):
        return jnp.broadcast_to(col, (CHUNK, LANES))

    def chunk_body(k, carry):
        c = k if direction == 0 else nchunk - 1 - k
        r0 = pl.multiple_of(c * CHUNK, CHUNK)
        rows = pl.ds(r0, CHUNK)
        dt = dt_ref[0, rows, :]
        dta = dt * a_row
        p0 = dta.astype(BF16)
        r1 = dta - p0.astype(F32)
        p1 = r1.astype(BF16)
        p2 = (r1 - p1.astype(F32)).astype(BF16)
        cc = _dot(tri, jnp.concatenate([p0, p1, p2], axis=1))
        cum = cc[:, :LANES] + cc[:, LANES:2 * LANES] + cc[:, 2 * LANES:]
        total = cum[CHUNK - 1:CHUNK, :] if direction == 0 else cum[0:1, :]
        cum_t = cum.T
        dt_t = dt.T
        e_tot = jnp.exp(total)
        w_end = jnp.exp(total - cum) * dt

        for g in range(SSD_GROUPS):
            b_g = xbc_ref[0, rows, d_inner + g * nstate:d_inner + (g + 1) * nstate]
            bg_t = b_g.astype(F32).T.astype(BF16)
            if want_y:
                c_off = d_inner + SSD_GROUPS * nstate
                c_g = xbc_ref[0, rows, c_off + g * nstate:c_off + (g + 1) * nstate]
                cb = lax.dot_general(c_g, b_g, (((1,), (1,)), ((), ())), preferred_element_type=F32)
            for q in range(npair):
                h1 = direction * nheads + g * (2 * npair) + 2 * q
                h2 = h1 + 1
                col0 = g * gcols + q * LANES
                xp = xbc_ref[0, rows, col0:col0 + LANES]
                sp = s_ref[g, :, q * LANES:(q + 1) * LANES]
                if want_y:
                    a1 = bcast(cum[:, h1:h1 + 1])
                    a2 = bcast(cum[:, h2:h2 + 1])
                    l1 = jnp.where(mask, jnp.exp(a1 - cum_t[h1:h1 + 1, :]), 0.0)
                    l2 = jnp.where(mask, jnp.exp(a2 - cum_t[h2:h2 + 1, :]), 0.0)
                    m1 = (cb * l1 * dt_t[h1:h1 + 1, :]).astype(BF16)
                    m2 = (cb * l2 * dt_t[h2:h2 + 1, :]).astype(BF16)
                    zero = jnp.zeros_like(xp)
                    rhs = jnp.concatenate([jnp.where(lo_half, xp, zero), jnp.where(lo_half, zero, xp)], axis=0)
                    yd = _dot(jnp.concatenate([m1, m2], axis=1), rhs)
                    yo = _dot(c_g, sp.astype(BF16)) * jnp.exp(jnp.where(lo_half, a1, a2))
                    ybuf_ref[:, col0:col0 + LANES] = yd + yo
                wp = jnp.where(lo_half, bcast(w_end[:, h1:h1 + 1]), bcast(w_end[:, h2:h2 + 1]))
                xw = (xp.astype(F32) * wp).astype(BF16)
                dec = jnp.where(lo_row, e_tot[:, h1:h1 + 1], e_tot[:, h2:h2 + 1])
                s_ref[g, :, q * LANES:(q + 1) * LANES] = sp * dec + _dot(bg_t, xw)

        if want_y and not finalize:
            out_ref[0, rows, :] = ybuf_ref[...].astype(BF16)
        if finalize:
            y = ybuf_ref[...] + yf_ref[0, rows, :].astype(F32)
            y = y + dsk_ref[...] * xbc_ref[0, rows, 0:d_inner].astype(F32)
            zz = z_ref[0, rows, :].astype(F32)
            y = y * (zz * _sigmoid(zz))
            for g in range(SSD_GROUPS):
                yg = y[:, g * gcols:(g + 1) * gcols]
                yn = yg * lax.rsqrt(jnp.mean(yg * yg, axis=-1, keepdims=True) + EPS)
                out_ref[0, rows, g * gcols:(g + 1) * gcols] = (yn * ng_ref[:, g * gcols:(g + 1) * gcols]).astype(BF16)
        return carry

    lax.fori_loop(0, nchunk, chunk_body, 0)

    if not want_y:
        @pl.when(step == pl.num_programs(1) - 1)
        def _():
            out_ref[0] = s_ref[...]


def _ssd_call(xbc, dt, alog, h0, yf, z, dsk, ng, *, direction, tb, want_y, d_inner):
    b, l, ncol = xbc.shape
    nblk = l // tb
    has_h0 = h0 is not None
    finalize = yf is not None
    gcols = d_inner // SSD_GROUPS

    def blk(bb, s):
        return (bb, s if direction == 0 else nblk - 1 - s, 0)

    in_specs = [pl.BlockSpec((1, tb, ncol), blk),
                pl.BlockSpec((1, tb, LANES), blk),
                _const_spec(alog.shape)]
    args = [xbc, dt, alog]
    if has_h0:
        in_specs.append(pl.BlockSpec((1, SSD_GROUPS, SSD_STATE, gcols), lambda bb, s: (bb, 0, 0, 0)))
        args.append(h0)
    if finalize:
        in_specs += [pl.BlockSpec((1, tb, d_inner), blk), pl.BlockSpec((1, tb, d_inner), blk),
                     _const_spec(dsk.shape), _const_spec(ng.shape)]
        args += [yf, z, dsk, ng]
    scratch = [pltpu.VMEM((SSD_GROUPS, SSD_STATE, gcols), F32)]
    if want_y:
        out_specs = pl.BlockSpec((1, tb, d_inner), blk)
        out_shape = jax.ShapeDtypeStruct((b, l, d_inner), BF16)
        scratch.append(pltpu.VMEM((CHUNK, d_inner), F32))
    else:
        out_specs = pl.BlockSpec((1, SSD_GROUPS, SSD_STATE, gcols), lambda bb, s: (bb, 0, 0, 0))
        out_shape = jax.ShapeDtypeStruct((b, SSD_GROUPS, SSD_STATE, gcols), F32)
    return pl.pallas_call(
        functools.partial(_ssd_kernel, direction=direction, nchunk=tb // CHUNK, want_y=want_y,
                          has_h0=has_h0, finalize=finalize, d_inner=d_inner),
        grid=(b, nblk),
        in_specs=in_specs,
        out_specs=out_specs,
        out_shape=out_shape,
        scratch_shapes=scratch,
        compiler_params=_cparams(("parallel", "arbitrary")),
        name=("ssd_" + ("lat" if want_y else "ctx") + ("_fwd" if direction == 0 else "_bwd")),
    )(*args)


def _post_kernel(x_ref, ma_ref, mb_ref, gm_ref, sf_ref, scf_ref, gf_ref, ng_ref, wa_ref, wb_ref,
                 w1_ref, w2_ref, *rest, tff, final):
    if final:
        fg_ref, o_ref = rest
    else:
        (o_ref,) = rest
    mix = _dot(ma_ref[0], wa_ref[...]) + _dot(mb_ref[0], wb_ref[...])
    lat = x_ref[0] + gm_ref[0] * mix
    hn = _modulate(lat, ng_ref[...], sf_ref[0], scf_ref[0]).astype(BF16)
    acc = jnp.zeros_like(lat)
    for c0 in range(0, w1_ref.shape[1], tff):
        a = jnp.maximum(_dot(hn, w1_ref[:, c0:c0 + tff]), 0.0)
        acc = acc + _dot((a * a).astype(BF16), w2_ref[c0:c0 + tff, :])
    out = lat + gf_ref[0] * acc
    if final:
        out = out * lax.rsqrt(jnp.mean(out * out, axis=-1, keepdims=True) + EPS) * fg_ref[...]
    o_ref[0] = out


def _post_call(x, ma, mb, gm, sf, scf, gf, ng, wa, wb, w1, w2, fg, *, tm):
    b, l, d = x.shape
    final = fg is not None
    tok = lambda bb, i: (bb, i, 0)
    mod = lambda bb, i: (bb, 0, 0)
    in_specs = [pl.BlockSpec((1, tm, d), tok),
                pl.BlockSpec((1, tm, ma.shape[2]), tok),
                pl.BlockSpec((1, tm, mb.shape[2]), tok),
                pl.BlockSpec((1, 1, d), mod), pl.BlockSpec((1, 1, d), mod),
                pl.BlockSpec((1, 1, d), mod), pl.BlockSpec((1, 1, d), mod),
                _const_spec(ng.shape), _const_spec(wa.shape), _const_spec(wb.shape),
                _const_spec(w1.shape), _const_spec(w2.shape)]
    args = [x, ma, mb, gm, sf, scf, gf, ng, wa, wb, w1, w2]
    if final:
        in_specs.append(_const_spec(fg.shape))
        args.append(fg)
    return pl.pallas_call(
        functools.partial(_post_kernel, tff=1024, final=final),
        grid=(b, l // tm),
        in_specs=in_specs,
        out_specs=pl.BlockSpec((1, tm, d), tok),
        out_shape=jax.ShapeDtypeStruct((b, l, d), F32),
        compiler_params=_cparams(("parallel", "parallel")),
        name="post_final" if final else "post",
    )(*args)


def _gelu_tanh(x):
    return x * (0.5 * (1.0 + jnp.tanh(math.sqrt(2.0 / math.pi) * (x + 0.044715 * (x * x * x)))))


def _odd_kernel(x_ref, xp_ref, xn_ref, sh_ref, sc_ref, g_ref, wu_ref, wv_ref, wp_ref, lng_ref, lnb_ref,
                ws_ref, bs_ref, wpool_ref, psc_ref, sgu_ref, pool_ref,
                hext_ref, u_ref, v_ref, praw_ref, pooled_ref, *, tm, seq):
    i = pl.program_id(1)
    last = pl.num_programs(1) - 1
    g, sh, sc = g_ref[...], sh_ref[0], sc_ref[0]
    h = _modulate(x_ref[0], g, sh, sc).astype(BF16)
    hp = _modulate(xp_ref[0, 0], g, sh, sc)
    hn = _modulate(xn_ref[0, 0], g, sh, sc)
    hext_ref[0:HALO] = jnp.where(i > 0, hp, 0.0).astype(BF16)
    hext_ref[HALO:HALO + tm] = h
    hext_ref[HALO + tm:] = jnp.where(i < last, hn, 0.0).astype(BF16)

    dsg = u_ref.shape[1]
    for c0 in range(0, dsg, 512):
        u_ref[:, c0:c0 + 512] = _gelu_tanh(_dot(h, wu_ref[:, c0:c0 + 512]))
        v_ref[:, c0:c0 + 512] = _gelu_tanh(_dot(h, wv_ref[:, c0:c0 + 512]))

    gw = dsg // SGU_GROUPS
    for r0 in range(0, tm, SGU_CHUNK):
        v = v_ref[r0:r0 + SGU_CHUNK, :]
        vc = v - jnp.mean(v, axis=-1, keepdims=True)
        vn = vc * lax.rsqrt(jnp.mean(vc * vc, axis=-1, keepdims=True) + EPS)
        vn = (vn * lng_ref[...] + lnb_ref[...]).astype(BF16)
        for gi in range(SGU_GROUPS):
            vm = _dot(ws_ref[gi], vn[:, gi * gw:(gi + 1) * gw]) + bs_ref[gi]
            sgu_ref[0, r0:r0 + SGU_CHUNK, gi * gw:(gi + 1) * gw] = (
                u_ref[r0:r0 + SGU_CHUNK, gi * gw:(gi + 1) * gw] * vm).astype(BF16)

    praw_ref[...] = _dot(hext_ref[...], wp_ref[...])
    pw = praw_ref.shape[1] // len(POOL_WINDOWS)
    for r0 in range(0, tm, 128):
        t = i * tm + r0 + lax.broadcasted_iota(jnp.int32, (128, pw), 0)
        for gi, w in enumerate(POOL_WINDOWS):
            r = w // 2
            cols = slice(gi * pw, (gi + 1) * pw)
            acc = praw_ref[pl.ds(HALO + r0 - r, 128), cols]
            for k in range(1, 2 * r + 1):
                acc = acc + praw_ref[pl.ds(HALO + r0 - r + k, 128), cols]
            cnt = (jnp.minimum(t + r + 1, seq) - jnp.maximum(t - r, 0)).astype(F32)
            pooled_ref[r0:r0 + 128, cols] = (acc / cnt - praw_ref[pl.ds(HALO + r0, 128), cols]).astype(BF16)
    pool_ref[0] = (_dot(pooled_ref[...], wpool_ref[...]) * psc_ref[...]).astype(BF16)


def _odd_call(x, shift, scale, g, wu, wv, wp, lng, lnb, ws, bs, wpool, psc, *, tm):
    b, l, d = x.shape
    th = tm // HALO
    nh = l // HALO
    x4 = x.reshape(b, nh, HALO, d)
    tok = lambda bb, i: (bb, i, 0)
    mod = lambda bb, i: (bb, 0, 0)
    dsg, dpool = wu.shape[1], wp.shape[1]
    in_specs = [
        pl.BlockSpec((1, tm, d), tok),
        pl.BlockSpec((1, 1, HALO, d), lambda bb, i: (bb, jnp.maximum(i * th - 1, 0), 0, 0)),
        pl.BlockSpec((1, 1, HALO, d), lambda bb, i: (bb, jnp.minimum((i + 1) * th, nh - 1), 0, 0)),
        pl.BlockSpec((1, 1, d), mod), pl.BlockSpec((1, 1, d), mod),
    ] + [_const_spec(a.shape) for a in (g, wu, wv, wp, lng, lnb, ws, bs, wpool, psc)]
    return pl.pallas_call(
        functools.partial(_odd_kernel, tm=tm, seq=l),
        grid=(b, l // tm),
        in_specs=in_specs,
        out_specs=[pl.BlockSpec((1, tm, dsg), tok), pl.BlockSpec((1, tm, dpool), tok)],
        out_shape=[jax.ShapeDtypeStruct((b, l, dsg), BF16), jax.ShapeDtypeStruct((b, l, dpool), BF16)],
        scratch_shapes=[pltpu.VMEM((tm + 2 * HALO, d), BF16),
                        pltpu.VMEM((tm, dsg), F32),
                        pltpu.VMEM((tm, dsg), F32),
                        pltpu.VMEM((tm + 2 * HALO, dpool), F32),
                        pltpu.VMEM((tm, dpool), BF16)],
        compiler_params=_cparams(("parallel", "parallel")),
        name="odd_mixer",
    )(x, x4, x4, shift, scale, g, wu, wv, wp, lng, lnb, ws, bs, wpool, psc)


def kernel(x, c, ctx, c_ctx, ada_w, ada_b, mix_norm_g, ffn_norm_g, ffn_w1, ffn_w2,
           ev_w_in, ev_conv_w, ev_conv_b, ev_dt_bias, ev_a_log, ev_d_skip, ev_ssd_norm_g, ev_w_out,
           od_w_in, od_sgu_ln_g, od_sgu_ln_b, od_w_s, od_b_s, od_pool_w, od_pool_scale, od_w_out,
           final_g):
    b, l, d = x.shape
    depth = ada_w.shape[0]
    assert depth == 2 and ev_w_in.shape[0] == 1 and od_w_in.shape[0] == 1
    nheads = ev_a_log.shape[2]
    d_inner = nheads * SSD_HEAD_DIM
    conv_dim = d_inner + 2 * SSD_GROUPS * SSD_STATE
    four_dim = ev_w_in.shape[2] - d_inner - conv_dim - 2 * nheads
    assert 2 * nheads <= LANES

    rows = -(-(b + 1) // 8) * 8
    cond = jnp.zeros((rows, d), F32).at[:b].set(c).at[b].set(c_ctx)
    mod = _ada_call(cond, ada_w, ada_b)
    lat_mod = mod[:, :b].reshape(depth, b, 1, 6, d)
    s_m, sc_m, g_m, s_f, sc_f, g_f = [lat_mod[:, :, :, j, :] for j in range(6)]
    ctx_mod = mod[0, b:b + 1].reshape(1, 1, 6, d)
    cs_m, csc_m = ctx_mod[:, :, 0, :], ctx_mod[:, :, 1, :]

    row = lambda v: v.reshape(1, -1).astype(F32)

    w_in = ev_w_in[0].astype(BF16)
    wf = w_in[:, :four_dim]
    wz = w_in[:, four_dim:four_dim + d_inner]
    wx = w_in[:, four_dim + d_inner:four_dim + d_inner + conv_dim]
    dt_cols = 2 * nheads
    wd = jnp.zeros((d, LANES), BF16).at[:, :dt_cols].set(w_in[:, four_dim + d_inner + conv_dim:])
    dtb = jnp.zeros((1, LANES), F32).at[0, :dt_cols].set(ev_dt_bias[0].reshape(-1))
    alog = jnp.zeros((1, LANES), F32).at[0, :dt_cols].set(ev_a_log[0].reshape(-1))
    cw, cb = ev_conv_w[0], row(ev_conv_b[0])
    g0 = row(mix_norm_g[0])

    f_in, z, xbc, dt = _evin_call(x, s_m[0], sc_m[0], g0, wf, wz, wx, wd, cw, cb, dtb, tm=512, with_fz=True)
    cxbc, cdt = _evin_call(ctx, cs_m, csc_m, g0, None, None, wx, wd, cw, cb, dtb,
                           tm=ctx.shape[1], with_fz=False)

    cmat, smat, wc = _dft_tables(l, four_dim, FOURIER_GROUPS)
    four = _dft_call(f_in, cmat, smat, wc, tk=512)

    dsk = jnp.repeat(ev_d_skip[0].astype(F32), SSD_HEAD_DIM).reshape(1, d_inner)
    ng = row(ev_ssd_norm_g[0])
    ssd = functools.partial(_ssd_call, d_inner=d_inner)
    hf = ssd(cxbc, cdt, alog, None, None, None, None, None, direction=0, tb=ctx.shape[1], want_y=False)
    hb = ssd(cxbc, cdt, alog, None, None, None, None, None, direction=1, tb=ctx.shape[1], want_y=False)
    yf = ssd(xbc, dt, alog, hf, None, None, None, None, direction=0, tb=512, want_y=True)
    y = ssd(xbc, dt, alog, hb, yf, z, dsk, ng, direction=1, tb=512, want_y=True)

    w_out = ev_w_out[0].astype(BF16)
    lat = _post_call(x, four, y, g_m[0], s_f[0], sc_f[0], g_f[0], row(ffn_norm_g[0]),
                     w_out[:four_dim], w_out[four_dim:], ffn_w1[0].astype(BF16), ffn_w2[0].astype(BF16),
                     None, tm=512)

    dsg = od_sgu_ln_g.shape[1]
    w_in1 = od_w_in[0].astype(BF16)
    wpool = jax.scipy.linalg.block_diag(*[od_pool_w[0, gi] for gi in range(od_pool_w.shape[1])]).astype(BF16)
    sgu, pool = _odd_call(lat, s_m[1], sc_m[1], row(mix_norm_g[1]),
                          w_in1[:, :dsg], w_in1[:, dsg:2 * dsg], w_in1[:, 2 * dsg:],
                          row(od_sgu_ln_g[0]), row(od_sgu_ln_b[0]),
                          od_w_s[0].astype(BF16), od_b_s[0].astype(F32)[:, :, None],
                          wpool, row(od_pool_scale[0]), tm=512)
    w_out1 = od_w_out[0].astype(BF16)
    return _post_call(lat, sgu, pool, g_m[1], s_f[1], sc_f[1], g_f[1], row(ffn_norm_g[1]),
                      w_out1[:dsg], w_out1[dsg:], ffn_w1[1].astype(BF16), ffn_w2[1].astype(BF16),
                      row(final_g), tm=512)
```
